```python
import functools
import jax, jax.numpy as jnp
from jax import lax
import numpy as np

D_MODEL = 1024
BATCH = 16
SEQ = 2048
DEPTH = 1
DEC_BATCH = 32
DEC_SEQ = 1
PAST_LEN = 16384
PAGE_SIZE = 128

A_GROUPS = 4
A_GROUP_W = 128
A_WIDTH = A_GROUPS * A_GROUP_W
CHUNK = 128
B_HEADS = 8
B_HEAD_DIM = 64
B_WIDTH = B_HEADS * B_HEAD_DIM
MOBA_BLOCK = 256
MOBA_TOPK = 3
Q_CHUNK = 8
MIX_WIDTH = A_WIDTH + B_WIDTH
IN_WIDTH = 3 * A_WIDTH + 4 * B_WIDTH
SPLIT_AT = (A_WIDTH, 2 * A_WIDTH, 3 * A_WIDTH, 3 * A_WIDTH + B_WIDTH,
            3 * A_WIDTH + 2 * B_WIDTH, 3 * A_WIDTH + 3 * B_WIDTH)
EPS = 1e-6
NEG = -1e30

kernel_name = "hymba_sgu_moba_alibi_step"


def _rms(x, g):
    xf = x.astype(jnp.float32)
    y = xf * lax.rsqrt(jnp.mean(xf * xf, axis=-1, keepdims=True) + EPS)
    return (y * g.astype(jnp.float32)).astype(x.dtype)


def alibi_slopes():
    return jnp.exp2(-8.0 * jnp.arange(1, B_HEADS + 1, dtype=jnp.float32) / B_HEADS)


def chunk_mlp(u, v, v_gain, w_s, b_s):
    B, T, _ = u.shape
    L = min(T, CHUNK)
    nc = T // L
    vn = _rms(jax.nn.gelu(v).reshape(B, T, A_GROUPS, A_GROUP_W), v_gain)
    w = jnp.tril(w_s[:, :L, :L])
    s = jnp.einsum('gts,bnsgc->bntgc', w, vn.reshape(B, nc, L, A_GROUPS, A_GROUP_W))
    s = s + jnp.swapaxes(b_s[:, :L], 0, 1)[None, None, :, :, None]
    out = jax.nn.gelu(u) * s.reshape(B, T, A_WIDTH)
    return out, vn.reshape(B, T, A_WIDTH)


def moba_prompt(q, k, v):
    f32 = jnp.float32
    B, S, H, D = q.shape
    nb = -(-S // MOBA_BLOCK)
    pad = nb * MOBA_BLOCK - S
    kp = jnp.pad(k, ((0, 0), (0, pad), (0, 0), (0, 0)))
    vp = jnp.pad(v, ((0, 0), (0, pad), (0, 0), (0, 0)))
    kblk = kp.reshape(B, nb, MOBA_BLOCK, H, D).transpose(0, 3, 1, 2, 4)
    vblk = vp.reshape(B, nb, MOBA_BLOCK, H, D).transpose(0, 3, 1, 2, 4)
    qf = jnp.swapaxes(q, 1, 2).astype(f32)
    pos = jnp.arange(S)
    qblk = pos // MOBA_BLOCK
    own = jnp.broadcast_to(qblk[:, None], (B, H, S, 1))
    topk = min(MOBA_TOPK, nb - 1)
    if topk > 0:
        kmean = jnp.mean(kblk.astype(f32), axis=3)
        gate = jnp.einsum('bhsd,bhnd->bhsn', qf, kmean)
        fully_past = jnp.arange(nb)[None, :] < qblk[:, None]
        gate = jnp.where(fully_past, gate, NEG)
        _, sel = lax.top_k(gate, topk)
        sel_ok = sel < qblk[:, None]
        blocks = jnp.concatenate([sel, own], axis=-1)
        ok = jnp.concatenate([sel_ok, jnp.ones((B, H, S, 1), bool)], axis=-1)
    else:
        blocks = own
        ok = jnp.ones((B, H, S, 1), bool)
    nq = S // Q_CHUNK
    slopes = alibi_slopes()[None, :, None, None, None]
    scale = D ** -0.5
    bi = jnp.arange(B)[:, None, None, None]
    hi = jnp.arange(H)[None, :, None, None]

    def to_chunks(a):
        return jnp.moveaxis(a.reshape(B, H, nq, Q_CHUNK, *a.shape[3:]), 2, 0)

    def step(args):
        qc, bc, okc, tc = args
        kg = kblk[bi, hi, bc]
        vg = vblk[bi, hi, bc]
        s = jnp.einsum('bhqd,bhqnjd->bhqnj', qc, kg.astype(f32)) * scale
        kpos = bc[..., None] * MOBA_BLOCK + jnp.arange(MOBA_BLOCK)
        dist = (tc[None, None, :, None, None] - kpos).astype(f32)
        s = jnp.where(okc[..., None] & (dist >= 0), s - slopes * dist, NEG)
        p = jax.nn.softmax(s.reshape(B, H, Q_CHUNK, -1), axis=-1).reshape(s.shape)
        return jnp.einsum('bhqnj,bhqnjd->bhqd', p.astype(vg.dtype), vg)

    o = lax.map(step, (to_chunks(qf), to_chunks(blocks), to_chunks(ok), pos.reshape(nq, Q_CHUNK)))
    o = jnp.moveaxis(o, 0, 2).reshape(B, H, S, D)
    return jnp.swapaxes(o, 1, 2).reshape(B, S, H * D).astype(q.dtype)


def moba_sample(q, k, v, cache_k, cache_v, page_table):
    f32 = jnp.float32
    Bd, T, H, D = q.shape
    n_pages = page_table.shape[1]
    past = n_pages * PAGE_SIZE
    ppb = MOBA_BLOCK // PAGE_SIZE
    n_full = past // MOBA_BLOCK
    own_first = n_full * ppb
    slopes = alibi_slopes()[None, :, None, None]
    scale = D ** -0.5
    qf = jnp.swapaxes(q, 1, 2).astype(f32)
    tpos = past + jnp.arange(T)
    scores, values = [], []
    s_new = jnp.einsum('bhtd,bjhd->bhtj', qf, k.astype(f32)) * scale
    dist = (tpos[:, None] - tpos[None, :]).astype(f32)
    scores.append(jnp.where(dist >= 0, s_new - slopes * dist, NEG))
    values.append(('bhtj,bjhd->bhtd', v))
    if own_first < n_pages:
        pages = page_table[:, own_first:]
        ko = cache_k[pages].reshape(Bd, -1, H, D)
        vo = cache_v[pages].reshape(Bd, -1, H, D)
        kpos = own_first * PAGE_SIZE + jnp.arange(ko.shape[1])
        s = jnp.einsum('bhtd,bjhd->bhtj', qf, ko.astype(f32)) * scale
        scores.append(s - slopes * (tpos[:, None] - kpos[None, :]).astype(f32))
        values.append(('bhtj,bjhd->bhtd', vo))
    topk = min(MOBA_TOPK, n_full)
    if topk > 0:
        k_full = cache_k[page_table[:, :own_first]].astype(f32).reshape(Bd, n_full, MOBA_BLOCK, H, D)
        kmean = jnp.mean(k_full, axis=2)
        gate = jnp.einsum('bhtd,bnhd->bhtn', qf, kmean)
        _, sel = lax.top_k(gate, topk)
        pidx = sel[..., None] * ppb + jnp.arange(ppb)
        pages = page_table[jnp.arange(Bd)[:, None, None, None, None], pidx]
        hidx = jnp.arange(H)[None, :, None, None, None]
        kg = cache_k[pages, :, hidx].reshape(Bd, H, T, topk * MOBA_BLOCK, D)
        vg = cache_v[pages, :, hidx].reshape(Bd, H, T, topk * MOBA_BLOCK, D)
        kpos = (sel[..., None] * MOBA_BLOCK + jnp.arange(MOBA_BLOCK)).reshape(Bd, H, T, -1)
        s = jnp.einsum('bhtd,bhtkd->bhtk', qf, kg.astype(f32)) * scale
        scores.append(s - slopes * (tpos[None, None, :, None] - kpos).astype(f32))
        values.append(('bhtk,bhtkd->bhtd', vg))
    sizes = [s.shape[-1] for s in scores]
    p = jax.nn.softmax(jnp.concatenate(scores, axis=-1), axis=-1)
    terms, off = [], 0
    for (spec, val), n in zip(values, sizes):
        terms.append(jnp.einsum(spec, p[..., off:off + n].astype(val.dtype), val))
        off += n
    out = functools.reduce(jnp.add, terms)
    return jnp.swapaxes(out, 1, 2).reshape(Bd, T, H * D).astype(q.dtype)


def _layer(x, attend, g_pre, w_in, sgu_gain, sgu_w, sgu_b, g_branch_a, g_branch_b, w_out, g_post):
    B, T, _ = x.shape
    h = _rms(x, g_pre)
    z = jnp.einsum('btd,df->btf', h, w_in)
    ua, va, ga, qb, kb, vb, gb = jnp.split(z, SPLIT_AT, axis=-1)
    ya, v_rows = chunk_mlp(ua, va, sgu_gain, sgu_w, sgu_b)
    kh = kb.reshape(B, T, B_HEADS, B_HEAD_DIM)
    vh = vb.reshape(B, T, B_HEADS, B_HEAD_DIM)
    yb = attend(qb.reshape(B, T, B_HEADS, B_HEAD_DIM), kh, vh)
    ya = _rms(ya, g_branch_a) * jax.nn.silu(ga)
    yb = _rms(yb, g_branch_b) * jax.nn.silu(gb)
    y = jnp.einsum('btf,fd->btd', jnp.concatenate([ya, yb], axis=-1), w_out)
    return x + _rms(y, g_post), kh, vh, v_rows


def setup_inputs(seed: int = 0) -> dict:
    key = jax.random.key(seed)
    ks = jax.random.split(key, 16)
    n_pages = PAST_LEN // PAGE_SIZE
    n_used = DEC_BATCH * n_pages
    n_phys = (n_used * 5 + 3) // 4
    perm = jax.random.permutation(ks[0], n_phys)
    page_table = perm[:n_used].reshape(DEC_BATCH, n_pages).astype(jnp.int32)
    nrm = jax.random.normal
    return {
        "x_prompt": nrm(ks[1], (BATCH, SEQ, D_MODEL), jnp.float32),
        "x_sample": nrm(ks[2], (DEC_BATCH, DEC_SEQ, D_MODEL), jnp.float32),
        "cache_k": nrm(ks[3], (DEPTH, n_phys, PAGE_SIZE, B_HEADS, B_HEAD_DIM), jnp.float32),
        "cache_v": nrm(ks[4], (DEPTH, n_phys, PAGE_SIZE, B_HEADS, B_HEAD_DIM), jnp.float32),
        "page_table": page_table,
        "g_pre": 1.0 + 0.05 * nrm(ks[5], (DEPTH, D_MODEL), jnp.float32),
        "w_in": nrm(ks[6], (DEPTH, D_MODEL, IN_WIDTH), jnp.float32) * D_MODEL ** -0.5,
        "sgu_gain": 1.0 + 0.05 * nrm(ks[7], (DEPTH, A_GROUPS, A_GROUP_W), jnp.float32),
        "sgu_w": nrm(ks[8], (DEPTH, A_GROUPS, CHUNK, CHUNK), jnp.float32) * CHUNK ** -0.5,
        "sgu_b": 1.0 + 0.1 * nrm(ks[9], (DEPTH, A_GROUPS, CHUNK), jnp.float32),
        "g_branch_a": 1.0 + 0.05 * nrm(ks[10], (DEPTH, A_WIDTH), jnp.float32),
        "g_branch_b": 1.0 + 0.05 * nrm(ks[11], (DEPTH, B_WIDTH), jnp.float32),
        "w_out": nrm(ks[12], (DEPTH, MIX_WIDTH, D_MODEL), jnp.float32) * MIX_WIDTH ** -0.5,
        "g_post": 1.0 + 0.05 * nrm(ks[13], (DEPTH, D_MODEL), jnp.float32),
    }


def reference(x_prompt, x_sample, cache_k, cache_v, page_table, g_pre, w_in, sgu_gain, sgu_w,
              sgu_b, g_branch_a, g_branch_b, w_out, g_post):
    yp, ys = x_prompt, x_sample
    kp_l, vp_l, ks_l, vs_l, sv_l = [], [], [], [], []
    for l in range(DEPTH):
        params = (g_pre[l], w_in[l], sgu_gain[l], sgu_w[l], sgu_b[l],
                  g_branch_a[l], g_branch_b[l], w_out[l], g_post[l])
        yp, kp, vp, _ = _layer(yp, moba_prompt, *params)
        attend_s = functools.partial(moba_sample, cache_k=cache_k[l], cache_v=cache_v[l],
                                     page_table=page_table)
        ys, kn, vn, sv = _layer(ys, attend_s, *params)
        kp_l.append(kp); vp_l.append(vp); ks_l.append(kn); vs_l.append(vn); sv_l.append(sv)
    k_prompt = jnp.stack(kp_l, 0)
    v_prompt = jnp.stack(vp_l, 0)
    k_sample = jnp.stack(ks_l, 0)
    v_sample = jnp.stack(vs_l, 0)
    sgu_v_sample = jnp.stack(sv_l, 0)
    return (yp, ys, k_prompt, v_prompt, k_sample, v_sample, sgu_v_sample)
```

```python
import functools
import math

import jax
import jax.numpy as jnp
from jax import lax
from jax.experimental import pallas as pl
from jax.experimental.pallas import tpu as pltpu

F32 = jnp.float32
BF16 = jnp.bfloat16
HIGHEST = lax.Precision.HIGHEST

D_MODEL = 1024
A_GROUPS = 4
A_GROUP_W = 128
A_WIDTH = A_GROUPS * A_GROUP_W
CHUNK = 128
B_HEADS = 8
B_HEAD_DIM = 64
B_WIDTH = B_HEADS * B_HEAD_DIM
MOBA_BLOCK = 256
MOBA_TOPK = 3
PAGE_SIZE = 128
PAGES_PER_BLOCK = MOBA_BLOCK // PAGE_SIZE
EPS = 1e-6
NEG = -1e30
QK_SCALE = B_HEAD_DIM ** -0.5
GROUP_W = 512

V7X_VMEM_LIMIT_BYTES = 56 * 1024 * 1024
PROMPT_TOKEN_TILE = 512
SAMPLE_PAGE_BUFFERS = 16

NT_DIMS = (((1,), (1,)), ((), ()))


def _gelu(x):
    c = math.sqrt(2.0 / math.pi)
    return 0.5 * x * (1.0 + jnp.tanh(c * (x + 0.044715 * (x * x * x))))


def _silu(x):
    return x / (1.0 + jnp.exp(-x))


def _rms(x, g):
    return x * lax.rsqrt(jnp.mean(x * x, axis=-1, keepdims=True) + EPS) * g


def _slope(h):
    return 2.0 ** (-(h + 1))


def _prompt_in_kernel(x_ref, gpre_ref, w5_ref, wkvt_ref, sgain_ref, sw_ref, sb_ref, gba_ref,
                      kt_ref, vt_ref, qp_ref, qf_ref, ya_ref, sgb_ref, km_ref):
    tm = x_ref.shape[1]
    h = _rms(x_ref[0], gpre_ref[...])
    hb = h.astype(BF16)

    def proj(c):
        return jnp.dot(hb, w5_ref[:, c * GROUP_W:(c + 1) * GROUP_W], preferred_element_type=F32)

    kvt = lax.dot_general(wkvt_ref[...], hb, NT_DIMS, preferred_element_type=F32)
    kt_ref[0] = kvt[:B_WIDTH]
    vt_ref[0] = kvt[B_WIDTH:]

    nblk = tm // MOBA_BLOCK
    hm = [jnp.mean(h[j * MOBA_BLOCK:(j + 1) * MOBA_BLOCK], axis=0, keepdims=True) for j in range(nblk)]
    hm = jnp.concatenate(hm + [jnp.zeros((8 - nblk, D_MODEL), F32)], axis=0)
    hm_hi = hm.astype(BF16)
    hm_lo = (hm - hm_hi.astype(F32)).astype(BF16)
    wk = wkvt_ref[0:B_WIDTH, :]
    km_ref[0, 0] = (lax.dot_general(hm_hi, wk, NT_DIMS, preferred_element_type=F32)
                    + lax.dot_general(hm_lo, wk, NT_DIMS, preferred_element_type=F32))

    q = proj(3)
    qf_ref[0] = q
    lo = lax.broadcasted_iota(jnp.int32, (tm, 128), 1) < B_HEAD_DIM
    for hp in range(B_HEADS // 2):
        tile = q[:, hp * 128:(hp + 1) * 128] * QK_SCALE
        qp_ref[0, 2 * hp] = jnp.where(lo, tile, 0.0).astype(BF16)
        qp_ref[0, 2 * hp + 1] = jnp.where(lo, pltpu.roll(tile, B_HEAD_DIM, 1), 0.0).astype(BF16)

    gv = _gelu(proj(1))
    tri = (lax.broadcasted_iota(jnp.int32, (CHUNK, CHUNK), 0)
           >= lax.broadcasted_iota(jnp.int32, (CHUNK, CHUNK), 1))
    s_cols = []
    for g in range(A_GROUPS):
        vn = _rms(gv[:, g * A_GROUP_W:(g + 1) * A_GROUP_W], sgain_ref[g:g + 1, :]).astype(BF16)
        wg = jnp.where(tri, sw_ref[g], 0.0).astype(BF16)
        rows = [jnp.dot(wg, vn[n * CHUNK:(n + 1) * CHUNK], preferred_element_type=F32) + sb_ref[g]
                for n in range(tm // CHUNK)]
        s_cols.append(jnp.concatenate(rows, axis=0))
    s = jnp.concatenate(s_cols, axis=1)
    oa = _gelu(proj(0)) * s
    ya_ref[0] = _rms(oa, gba_ref[...]) * _silu(proj(2))
    sgb_ref[0] = _silu(proj(4))


def _prompt_in(x, g_pre, w5, wkvt, sgain, sw, sb, gba):
    B, S, _ = x.shape
    tm = min(PROMPT_TOKEN_TILE, S)
    nt = S // tm
    const2 = lambda b, m: (0, 0)
    const3 = lambda b, m: (0, 0, 0)
    tok = lambda b, m: (b, m, 0)
    out_shape = (
        jax.ShapeDtypeStruct((B, B_WIDTH, S), F32),
        jax.ShapeDtypeStruct((B, B_WIDTH, S), F32),
        jax.ShapeDtypeStruct((B, B_HEADS, S, 128), BF16),
        jax.ShapeDtypeStruct((B, S, B_WIDTH), F32),
        jax.ShapeDtypeStruct((B, S, A_WIDTH), F32),
        jax.ShapeDtypeStruct((B, S, B_WIDTH), F32),
        jax.ShapeDtypeStruct((B, nt, 8, B_WIDTH), F32),
    )
    return pl.pallas_call(
        _prompt_in_kernel,
        grid=(B, nt),
        in_specs=[
            pl.BlockSpec((1, tm, D_MODEL), tok),
            pl.BlockSpec((1, D_MODEL), const2),
            pl.BlockSpec(w5.shape, const2),
            pl.BlockSpec(wkvt.shape, const2),
            pl.BlockSpec(sgain.shape, const2),
            pl.BlockSpec(sw.shape, const3),
            pl.BlockSpec(sb.shape, const3),
            pl.BlockSpec((1, A_WIDTH), const2),
        ],
        out_specs=(
            pl.BlockSpec((1, B_WIDTH, tm), lambda b, m: (b, 0, m)),
            pl.BlockSpec((1, B_WIDTH, tm), lambda b, m: (b, 0, m)),
            pl.BlockSpec((1, B_HEADS, tm, 128), lambda b, m: (b, 0, m, 0)),
            pl.BlockSpec((1, tm, B_WIDTH), tok),
            pl.BlockSpec((1, tm, A_WIDTH), tok),
            pl.BlockSpec((1, tm, B_WIDTH), tok),
            pl.BlockSpec((1, 1, 8, B_WIDTH), lambda b, m: (b, m, 0, 0)),
        ),
        out_shape=out_shape,
        compiler_params=pltpu.CompilerParams(
            dimension_semantics=("arbitrary", "arbitrary"),
            vmem_limit_bytes=V7X_VMEM_LIMIT_BYTES),
        name="prompt_in",
    )(x, g_pre, w5, wkvt, sgain, sw, sb, gba)


def _prompt_attn_kernel(qf_ref, km_ref, qp_ref, kt_ref, vt_ref, ct_ref, o_ref, e_scr, *, topk):
    S = qp_ref.shape[2]
    nb = S // MOBA_BLOCK

    @pl.when(pl.program_id(1) == 0)
    def _():
        kmt = jnp.concatenate([km_ref[0]] * B_HEADS, axis=0)
        rh = lax.broadcasted_iota(jnp.int32, kmt.shape, 0) // 8
        ch = lax.broadcasted_iota(jnp.int32, kmt.shape, 1) // B_HEAD_DIM
        kmt = jnp.where(rh == ch, kmt, 0.0)
        gt = lax.dot_general(kmt, qf_ref[0], NT_DIMS, precision=HIGHEST,
                             preferred_element_type=F32)
        qb = lax.broadcasted_iota(jnp.int32, (8, S), 1) // MOBA_BLOCK
        jrow = lax.broadcasted_iota(jnp.int32, (8, S), 0)
        past = jrow < qb
        rows = [jnp.zeros((B_HEADS * 8, S), F32)]
        for h in range(B_HEADS):
            g = jnp.where(past, gt[h * 8:(h + 1) * 8], NEG)
            cnt = jnp.zeros((8, S), jnp.int32)
            for jp in range(8):
                gj = g[jp:jp + 1, :]
                beats = (gj > g) | ((gj == g) & (jp < jrow))
                cnt = cnt + beats.astype(jnp.int32)
            sel = past & (cnt < topk)
            bias = (_slope(h) * MOBA_BLOCK) * (jrow - qb).astype(F32)
            e = jnp.where(jrow == qb, 0.0, jnp.where(sel, bias, NEG))
            rows.append(jnp.where(jrow == 7, _slope(h), e))
        e_scr[...] = jnp.concatenate(rows, axis=0).T.astype(BF16)

    lane = lax.broadcasted_iota(jnp.int32, (S, 128), 1)
    qq = jnp.where(lane < B_HEAD_DIM, qp_ref[0, 0], e_scr[...])
    kk = jnp.concatenate([kt_ref[0].astype(BF16), ct_ref[0]], axis=0)
    vv = vt_ref[0].astype(BF16)
    tri = (lax.broadcasted_iota(jnp.int32, (MOBA_BLOCK, MOBA_BLOCK), 0)
           >= lax.broadcasted_iota(jnp.int32, (MOBA_BLOCK, MOBA_BLOCK), 1))
    for i in range(nb):
        lo, hi = i * MOBA_BLOCK, (i + 1) * MOBA_BLOCK
        qi = qq[lo:hi]
        sd = jnp.where(tri, jnp.dot(qi, kk[:, lo:hi], preferred_element_type=F32), NEG)
        m = jnp.max(sd, axis=1, keepdims=True)
        if i > 0:
            so = jnp.dot(qi, kk[:, :lo], preferred_element_type=F32)
            m = jnp.maximum(m, jnp.max(so, axis=1, keepdims=True))
        pd = jnp.exp(sd - m)
        l = jnp.sum(pd, axis=1, keepdims=True)
        acc = lax.dot_general(pd.astype(BF16), vv[:, lo:hi], NT_DIMS, preferred_element_type=F32)
        if i > 0:
            po = jnp.exp(so - m)
            l = l + jnp.sum(po, axis=1, keepdims=True)
            acc = acc + lax.dot_general(po.astype(BF16), vv[:, :lo], NT_DIMS,
                                        preferred_element_type=F32)
        o_ref[0, 0, lo:hi, :] = acc / l


def _score_constants(S):
    pos = jnp.arange(S)
    r = jnp.arange(B_HEADS * 8)
    hh = jnp.arange(B_HEADS)[:, None, None]
    rh, rj = (r // 8)[None, :, None], (r % 8)[None, :, None]
    blk, within = (pos // MOBA_BLOCK)[None, None, :], (pos % MOBA_BLOCK)[None, None, :]
    val = jnp.where(rj == 7, within, (blk == rj).astype(jnp.int32))
    return jnp.where(rh == hh, val, 0).astype(BF16)


def _prompt_attn(qf, km, qp, kt, vt):
    B, S, _ = qf.shape
    nb = S // MOBA_BLOCK
    ct = _score_constants(S)
    return pl.pallas_call(
        functools.partial(_prompt_attn_kernel, topk=min(MOBA_TOPK, nb - 1)),
        grid=(B, B_HEADS),
        in_specs=[
            pl.BlockSpec((1, S, B_WIDTH), lambda b, h: (b, 0, 0)),
            pl.BlockSpec((1, 8, B_WIDTH), lambda b, h: (b, 0, 0)),
            pl.BlockSpec((1, 1, S, 128), lambda b, h: (b, h, 0, 0)),
            pl.BlockSpec((1, B_HEAD_DIM, S), lambda b, h: (b, h, 0)),
            pl.BlockSpec((1, B_HEAD_DIM, S), lambda b, h: (b, h, 0)),
            pl.BlockSpec((1, B_HEADS * 8, S), lambda b, h: (h, 0, 0)),
        ],
        out_specs=pl.BlockSpec((1, 1, S, B_HEAD_DIM), lambda b, h: (b, h, 0, 0)),
        out_shape=jax.ShapeDtypeStruct((B, B_HEADS, S, B_HEAD_DIM), F32),
        scratch_shapes=[pltpu.VMEM((S, 128), BF16)],
        compiler_params=pltpu.CompilerParams(
            dimension_semantics=("arbitrary", "arbitrary"),
            vmem_limit_bytes=V7X_VMEM_LIMIT_BYTES),
        name="prompt_attn",
    )(qf, km, qp, kt, vt, ct)


def _prompt_out_kernel(x_ref, ya_ref, sgb_ref, yb_ref, gbb_ref, wout_ref, gpost_ref, o_ref):
    yb = jnp.concatenate([yb_ref[0, h] for h in range(B_HEADS)], axis=1)
    ybn = _rms(yb, gbb_ref[...]) * sgb_ref[0]
    mix = jnp.concatenate([ya_ref[0], ybn], axis=1).astype(BF16)
    y = jnp.dot(mix, wout_ref[...], preferred_element_type=F32)
    o_ref[0] = x_ref[0] + _rms(y, gpost_ref[...])


def _prompt_out(x, ya, sgb, yb, gbb, wout, gpost):
    B, S, _ = x.shape
    tm = min(PROMPT_TOKEN_TILE, S)
    const2 = lambda b, m: (0, 0)
    tok = lambda b, m: (b, m, 0)
    return pl.pallas_call(
        _prompt_out_kernel,
        grid=(B, S // tm),
        in_specs=[
            pl.BlockSpec((1, tm, D_MODEL), tok),
            pl.BlockSpec((1, tm, A_WIDTH), tok),
            pl.BlockSpec((1, tm, B_WIDTH), tok),
            pl.BlockSpec((1, B_HEADS, tm, B_HEAD_DIM), lambda b, m: (b, 0, m, 0)),
            pl.BlockSpec((1, B_WIDTH), const2),
            pl.BlockSpec(wout.shape, const2),
            pl.BlockSpec((1, D_MODEL), const2),
        ],
        out_specs=pl.BlockSpec((1, tm, D_MODEL), tok),
        out_shape=jax.ShapeDtypeStruct((B, S, D_MODEL), F32),
        compiler_params=pltpu.CompilerParams(
            dimension_semantics=("arbitrary", "arbitrary"),
            vmem_limit_bytes=V7X_VMEM_LIMIT_BYTES),
        name="prompt_out",
    )(x, ya, sgb, yb, gbb, wout, gpost)


def _sample_in_kernel(x_ref, gpre_ref, w_ref, sgain_ref, w00_ref, b0_ref, gba_ref,
                      q_ref, k_ref, v_ref, vn_ref, ya_ref, sgb_ref, z_scr):
    c = pl.program_id(0)
    h = _rms(x_ref[...], gpre_ref[...])
    z_scr[c] = jnp.dot(h, w_ref[...], precision=HIGHEST, preferred_element_type=F32)

    @pl.when(c == pl.num_programs(0) - 1)
    def _():
        q_ref[...] = z_scr[3]
        k_ref[...] = z_scr[4]
        v_ref[...] = z_scr[5]
        gv = _gelu(z_scr[1])
        vn = jnp.concatenate([_rms(gv[:, g * A_GROUP_W:(g + 1) * A_GROUP_W], 1.0)
                              for g in range(A_GROUPS)], axis=1) * sgain_ref[...]
        vn_ref[...] = vn
        oa = _gelu(z_scr[0]) * (vn * w00_ref[...] + b0_ref[...])
        ya_ref[...] = _rms(oa, gba_ref[...]) * _silu(z_scr[2])
        sgb_ref[...] = _silu(z_scr[6])


def _sample_in(x, g_pre, w_in, sgain_row, w00_row, b0_row, gba):
    Bd = x.shape[0]
    ngroups = w_in.shape[1] // GROUP_W
    const = lambda c: (0, 0)
    row = pl.BlockSpec((1, GROUP_W), const)
    out = pl.BlockSpec((Bd, GROUP_W), const)
    return pl.pallas_call(
        _sample_in_kernel,
        grid=(ngroups,),
        in_specs=[pl.BlockSpec((Bd, D_MODEL), const), pl.BlockSpec((1, D_MODEL), const),
                  pl.BlockSpec((D_MODEL, GROUP_W), lambda c: (0, c)), row, row, row, row],
        out_specs=(out,) * 6,
        out_shape=(jax.ShapeDtypeStruct((Bd, GROUP_W), F32),) * 6,
        scratch_shapes=[pltpu.VMEM((ngroups, Bd, GROUP_W), F32)],
        compiler_params=pltpu.CompilerParams(dimension_semantics=("arbitrary",)),
        name="sample_in",
    )(x, g_pre, w_in, sgain_row, w00_row, b0_row, gba)


def _sample_select_kernel(pt_ref, q_ref, ck_ref, sel_ref, buf, sem, kmt_scr, *, topk):
    Bd = q_ref.shape[0]
    total = pt_ref.shape[0]
    n_pages = total // Bd
    n_full = n_pages // PAGES_PER_BLOCK
    nbuf = buf.shape[0]
    groups_per_b = n_pages // nbuf

    def page_copy(idx, slot):
        return pltpu.make_async_copy(ck_ref.at[pt_ref[idx]], buf.at[slot], sem.at[slot])

    for s in range(nbuf):
        page_copy(s, s).start()

    lane = lax.broadcasted_iota(jnp.int32, kmt_scr.shape, 1)
    lane8 = lax.broadcasted_iota(jnp.int32, (8, 128), 1)
    head_of_row = lax.broadcasted_iota(jnp.int32, (8, B_WIDTH), 0)
    head_of_col = lax.broadcasted_iota(jnp.int32, (8, B_WIDTH), 1) // B_HEAD_DIM

    def batch_body(b, carry):
        kmt_scr[...] = jnp.zeros(kmt_scr.shape, F32)

        def group_body(gi, carry2):
            base = (b * groups_per_b + gi) * nbuf
            for u in range(0, nbuf, PAGES_PER_BLOCK):
                acc = None
                for pp in range(PAGES_PER_BLOCK):
                    slot = u + pp
                    page_copy(base + slot, slot).wait()
                    page = buf[slot]
                    acc = page if acc is None else acc + page

                    @pl.when(base + slot + nbuf < total)
                    def _():
                        page_copy(base + slot + nbuf, slot).start()
                col = jnp.sum(acc, axis=1, keepdims=True) * (1.0 / MOBA_BLOCK)
                n = gi * (nbuf // PAGES_PER_BLOCK) + u // PAGES_PER_BLOCK
                kmt_scr[...] = jnp.where(lane == n, col, kmt_scr[...])
            return carry2

        lax.fori_loop(0, groups_per_b, group_body, 0)

        qbd = jnp.where(head_of_row == head_of_col, jnp.broadcast_to(q_ref[b], (8, B_WIDTH)), 0.0)
        gate = jnp.dot(qbd, kmt_scr[...], precision=HIGHEST, preferred_element_type=F32)
        g = jnp.where(lane8 < n_full, gate, -jnp.inf)
        selv = jnp.zeros((8, 128), jnp.int32)
        for t in range(topk):
            m = jnp.max(g, axis=1, keepdims=True)
            idx = jnp.min(jnp.where(g == m, lane8, 128), axis=1, keepdims=True)
            selv = jnp.where(lane8 == t, idx, selv)
            g = jnp.where(lane8 == idx, -jnp.inf, g)
        sel_ref[b] = selv
        return carry

    lax.fori_loop(0, Bd, batch_body, 0)


def _sample_select(pt_flat, q3, ckt, topk):
    Bd = q3.shape[0]
    grid_spec = pltpu.PrefetchScalarGridSpec(
        num_scalar_prefetch=1,
        grid=(1,),
        in_specs=[pl.BlockSpec(q3.shape, lambda i, pt: (0, 0, 0)),
                  pl.BlockSpec(memory_space=pl.ANY)],
        out_specs=pl.BlockSpec((Bd, 8, 128), lambda i, pt: (0, 0, 0)),
        scratch_shapes=[pltpu.VMEM((SAMPLE_PAGE_BUFFERS, B_WIDTH, PAGE_SIZE), F32),
                        pltpu.SemaphoreType.DMA((SAMPLE_PAGE_BUFFERS,)),
                        pltpu.VMEM((B_WIDTH, 128), F32)],
    )
    return pl.pallas_call(
        functools.partial(_sample_select_kernel, topk=topk),
        grid_spec=grid_spec,
        out_shape=jax.ShapeDtypeStruct((Bd, 8, 128), jnp.int32),
        compiler_params=pltpu.CompilerParams(dimension_semantics=("arbitrary",)),
        name="sample_select",
    )(pt_flat, q3, ckt)


def _sample_attn_kernel(pt_ref, sel_ref, q_ref, kn_ref, vn_ref, ck_ref, cv_ref, o_ref,
                        kbuf, vbuf, sem, *, topk, n_pages):
    b = pl.program_id(0)
    past = n_pages * PAGE_SIZE
    width = topk * MOBA_BLOCK

    copies = []
    for h in range(B_HEADS):
        for t in range(topk):
            blk = sel_ref[(b * B_HEADS + h) * topk + t]
            for pp in range(PAGES_PER_BLOCK):
                page = pt_ref[b * n_pages + blk * PAGES_PER_BLOCK + pp]
                off = (t * PAGES_PER_BLOCK + pp) * PAGE_SIZE
                rows = pl.ds(h * B_HEAD_DIM, B_HEAD_DIM)
                for src, dst, sidx in ((ck_ref, kbuf, 0), (cv_ref, vbuf, 1)):
                    cp = pltpu.make_async_copy(src.at[page, rows, :],
                                               dst.at[h, :, pl.ds(off, PAGE_SIZE)], sem.at[sidx, h])
                    cp.start()
                    copies.append(cp)
    for cp in copies:
        cp.wait()

    q = q_ref[0]
    kn = kn_ref[0]
    vn = vn_ref[0]
    lane = lax.broadcasted_iota(jnp.int32, (1, width), 1)
    outs = []
    for h in range(B_HEADS):
        hs = slice(h * B_HEAD_DIM, (h + 1) * B_HEAD_DIM)
        qh = q[:, hs] * QK_SCALE
        s = jnp.dot(jnp.broadcast_to(qh, (8, B_HEAD_DIM)), kbuf[h], precision=HIGHEST,
                    preferred_element_type=F32)[0:1]
        blkv = jnp.zeros((1, width), jnp.int32)
        for t in range(topk):
            blkv = jnp.where(lane // MOBA_BLOCK == t, sel_ref[(b * B_HEADS + h) * topk + t], blkv)
        kpos = blkv * MOBA_BLOCK + lane % MOBA_BLOCK
        s = s - _slope(h) * (past - kpos).astype(F32)
        s_new = jnp.sum(qh * kn[:, hs], axis=1, keepdims=True)
        m = jnp.maximum(jnp.max(s, axis=1, keepdims=True), s_new)
        p = jnp.exp(s - m)
        p_new = jnp.exp(s_new - m)
        l = jnp.sum(p, axis=1, keepdims=True) + p_new
        pv = lax.dot_general(jnp.broadcast_to(p, (8, width)), vbuf[h], NT_DIMS, precision=HIGHEST,
                             preferred_element_type=F32)[0:1]
        outs.append((pv + p_new * vn[:, hs]) / l)
    o_ref[0] = jnp.concatenate(outs, axis=1)


def _sample_attn(pt_flat, sel_flat, q3, k3, v3, ckt, cvt, topk, n_pages):
    Bd = q3.shape[0]
    vec = pl.BlockSpec((1, 1, B_WIDTH), lambda b, pt, sel: (b, 0, 0))
    grid_spec = pltpu.PrefetchScalarGridSpec(
        num_scalar_prefetch=2,
        grid=(Bd,),
        in_specs=[vec, vec, vec, pl.BlockSpec(memory_space=pl.ANY), pl.BlockSpec(memory_space=pl.ANY)],
        out_specs=vec,
        scratch_shapes=[pltpu.VMEM((B_HEADS, B_HEAD_DIM, topk * MOBA_BLOCK), F32),
                        pltpu.VMEM((B_HEADS, B_HEAD_DIM, topk * MOBA_BLOCK), F32),
                        pltpu.SemaphoreType.DMA((2, B_HEADS))],
    )
    return pl.pallas_call(
        functools.partial(_sample_attn_kernel, topk=topk, n_pages=n_pages),
        grid_spec=grid_spec,
        out_shape=jax.ShapeDtypeStruct((Bd, 1, B_WIDTH), F32),
        compiler_params=pltpu.CompilerParams(dimension_semantics=("arbitrary",)),
        name="sample_attn",
    )(pt_flat, sel_flat, q3, k3, v3, ckt, cvt)


def _sample_out_kernel(x_ref, ya_ref, sgb_ref, yb_ref, gbb_ref, wout_ref, gpost_ref, o_ref):
    ybn = _rms(yb_ref[...], gbb_ref[...]) * sgb_ref[...]
    mix = jnp.concatenate([ya_ref[...], ybn], axis=1)
    y = jnp.dot(mix, wout_ref[...], precision=HIGHEST, preferred_element_type=F32)
    o_ref[...] = x_ref[...] + _rms(y, gpost_ref[...])


def _sample_out(x, ya, sgb, yb, gbb, wout, gpost):
    return pl.pallas_call(
        _sample_out_kernel,
        out_shape=jax.ShapeDtypeStruct(x.shape, F32),
        name="sample_out",
    )(x, ya, sgb, yb, gbb, wout, gpost)


def kernel(x_prompt, x_sample, cache_k, cache_v, page_table, g_pre, w_in, sgu_gain, sgu_w, sgu_b,
           g_branch_a, g_branch_b, w_out, g_post):
    assert w_in.shape[0] == 1, "one layer"
    B, S, _ = x_prompt.shape
    Bd, T, _ = x_sample.shape
    n_pages = page_table.shape[1]
    n_phys = cache_k.shape[1]
    assert T == 1 and S % MOBA_BLOCK == 0 and S // MOBA_BLOCK <= 8
    assert n_pages % SAMPLE_PAGE_BUFFERS == 0 and n_pages // PAGES_PER_BLOCK <= 128
    topk_s = min(MOBA_TOPK, n_pages // PAGES_PER_BLOCK)

    w = w_in[0]
    cols = lambda c: w[:, c * GROUP_W:(c + 1) * GROUP_W]
    w5 = jnp.concatenate([cols(0), cols(1), cols(2), cols(3), cols(6)], axis=1).astype(BF16)
    wkvt = jnp.concatenate([cols(4), cols(5)], axis=1).T.astype(BF16)
    sb = jnp.broadcast_to(sgu_b[0][:, :, None], (A_GROUPS, CHUNK, A_GROUP_W))

    kt, vt, qp, qf, ya, sgb, km = _prompt_in(x_prompt, g_pre, w5, wkvt, sgu_gain[0], sgu_w[0], sb,
                                            g_branch_a)
    nblk = min(PROMPT_TOKEN_TILE, S) // MOBA_BLOCK
    km = km[:, :, :nblk, :].reshape(B, S // MOBA_BLOCK, B_WIDTH)
    km = jnp.pad(km, ((0, 0), (0, 8 - S // MOBA_BLOCK), (0, 0)))
    yb = _prompt_attn(qf, km, qp, kt, vt)
    y_prompt = _prompt_out(x_prompt, ya, sgb, yb, g_branch_b, w_out[0].astype(BF16), g_post)
    to_rows = lambda a: a.reshape(B, B_HEADS, B_HEAD_DIM, S).transpose(0, 3, 1, 2)[None]
    k_prompt, v_prompt = to_rows(kt), to_rows(vt)

    ckt = cache_k[0].transpose(0, 2, 3, 1).reshape(n_phys, B_WIDTH, PAGE_SIZE)
    cvt = cache_v[0].transpose(0, 2, 3, 1).reshape(n_phys, B_WIDTH, PAGE_SIZE)
    xs = x_sample.reshape(Bd, D_MODEL)
    rep = lambda a: jnp.repeat(a, A_GROUP_W)[None, :]
    qs, ks, vs, vns, yas, sgbs = _sample_in(xs, g_pre, w, sgu_gain[0].reshape(1, A_WIDTH),
                                            rep(sgu_w[0, :, 0, 0]), rep(sgu_b[0, :, 0]), g_branch_a)
    pt_flat = page_table.reshape(-1)
    to3 = lambda a: a.reshape(Bd, 1, B_WIDTH)
    sel = _sample_select(pt_flat, to3(qs), ckt, topk_s)
    sel_flat = sel[:, :, :topk_s].reshape(-1)
    ybs = _sample_attn(pt_flat, sel_flat, to3(qs), to3(ks), to3(vs), ckt, cvt, topk_s, n_pages)
    y_sample = _sample_out(xs, yas, sgbs, ybs.reshape(Bd, B_WIDTH), g_branch_b, w_out[0], g_post)

    heads = lambda a: a.reshape(1, Bd, 1, B_HEADS, B_HEAD_DIM)
    return (y_prompt, y_sample.reshape(Bd, 1, D_MODEL), k_prompt, v_prompt, heads(ks), heads(vs),
            vns.reshape(1, Bd, 1, A_WIDTH))
```
